```python
import jax, jax.numpy as jnp
from jax import lax
import numpy as np

D_MODEL = 1024
BATCH = 32
SEQ = 2048
DEPTH = 4

GM_WIDTH = D_MODEL
GM_GROUPS = 8
GM_CHUNK = 128
LRU_WIDTH = D_MODEL
LRU_HEADS = 8
LRU_CONV = 4
LRU_C = 8.0
HG_HEADS = 8
HG_EXPAND = D_MODEL // HG_HEADS
HG_HEAD_DIM = D_MODEL // HG_HEADS
HG_KEY_WIDTH = HG_HEADS * HG_EXPAND
HG_WIDTH = HG_HEADS * HG_HEAD_DIM
HG_CHUNK = 32
N_BRANCH = 3
EPS = 1e-6
SPLIT_SIZES = (GM_WIDTH, GM_WIDTH, GM_WIDTH,
               LRU_WIDTH, LRU_WIDTH,
               HG_KEY_WIDTH, HG_KEY_WIDTH, HG_WIDTH, HG_WIDTH,
               N_BRANCH * D_MODEL)
IN_COLS = sum(SPLIT_SIZES)
SPLIT_POINTS = [int(s) for s in np.cumsum(SPLIT_SIZES)[:-1]]

kernel_name = "hybrid_gmlp_rglru_hgrn2_gated_merge"


def rms_norm(x, g):
    xf = x.astype(jnp.float32)
    y = xf * lax.rsqrt(jnp.mean(xf * xf, axis=-1, keepdims=True) + EPS)
    return (y * g.astype(jnp.float32)).astype(x.dtype)


def layer_norm(x, g, b):
    xf = x.astype(jnp.float32)
    mu = jnp.mean(xf, axis=-1, keepdims=True)
    xc = xf - mu
    y = xc * lax.rsqrt(jnp.mean(xc * xc, axis=-1, keepdims=True) + EPS)
    return (y * g.astype(jnp.float32) + b.astype(jnp.float32)).astype(x.dtype)


def chunked_spatial_gating(u, v, ln_g, ln_b, w_s, b_s):
    bsz, seq, width = v.shape
    v = layer_norm(v, ln_g, ln_b)
    vc = v.reshape(bsz, seq // GM_CHUNK, GM_CHUNK, GM_GROUPS, width // GM_GROUPS)
    causal = jnp.tril(jnp.ones((GM_CHUNK, GM_CHUNK), dtype=bool))
    w = jnp.where(causal[None], w_s, jnp.zeros_like(w_s))
    mixed = jnp.einsum('gts,bnsgc->bntgc', w, vc) + b_s.T[None, None, :, :, None]
    return u * mixed.reshape(bsz, seq, width)


def causal_depthwise_conv(x, w, b):
    k, width = w.shape
    out = lax.conv_general_dilated(
        x, w[:, None, :], window_strides=(1,), padding=[(k - 1, 0)],
        dimension_numbers=('NWC', 'WIO', 'NWC'), feature_group_count=width)
    return out + b


def block_diag_linear(x, w, b):
    bsz, seq, width = x.shape
    nh = w.shape[0]
    xh = x.reshape(bsz, seq, nh, width // nh)
    return jnp.einsum('bshi,hij->bshj', xh, w).reshape(bsz, seq, width) + b


def rg_lru(x, w_r, b_r, w_i, b_i, lam):
    f32 = jnp.float32
    r = jax.nn.sigmoid(block_diag_linear(x, w_r, b_r).astype(f32))
    i = jax.nn.sigmoid(block_diag_linear(x, w_i, b_i).astype(f32))
    log_a = -LRU_C * r * jax.nn.softplus(-lam.astype(f32))
    a = jnp.exp(log_a)
    mult = jnp.sqrt(-jnp.expm1(2.0 * log_a))
    bx = mult * (i * x.astype(f32))

    def step(h, ab):
        a_t, b_t = ab
        h = a_t * h + b_t
        return h, h

    h0 = jnp.zeros((x.shape[0], x.shape[2]), f32)
    _, hs = lax.scan(step, h0, (jnp.swapaxes(a, 0, 1), jnp.swapaxes(bx, 0, 1)))
    return jnp.swapaxes(hs, 0, 1).astype(x.dtype)


def hgrn2_chunked(q, f_raw, v, lower_bound):
    f32 = jnp.float32
    bsz, seq, nh, dk = q.shape
    dv = v.shape[-1]
    nc = seq // HG_CHUNK
    lb = lower_bound.reshape(nh, dk).astype(f32)
    fr = f_raw.astype(f32)
    f = lb + (1.0 - lb) * jax.nn.sigmoid(fr)
    k = (1.0 - lb) * jax.nn.sigmoid(-fr)
    g = jnp.log(f)

    def to_chunks(t):
        return jnp.moveaxis(t.reshape(bsz, nc, HG_CHUNK, nh, t.shape[-1]), 1, 0)

    gcum = jnp.cumsum(to_chunks(g), axis=2)
    causal = jnp.tril(jnp.ones((HG_CHUNK, HG_CHUNK), dtype=bool))[None, :, :, None, None]

    def step(state, inp):
        qc, kc, vc, gc = inp
        o_inter = jnp.einsum('bthk,bhkv->bthv', qc * jnp.exp(gc), state)
        diff = gc[:, :, None] - gc[:, None, :]
        decay = jnp.exp(jnp.where(causal, diff, -jnp.inf))
        scores = jnp.einsum('bthk,bshk,btshk->bhts', qc, kc, decay)
        o_intra = jnp.einsum('bhts,bshv->bthv', scores, vc)
        g_last = gc[:, -1]
        k_dec = kc * jnp.exp(g_last[:, None] - gc)
        state = state * jnp.exp(g_last)[..., None] + jnp.einsum('bshk,bshv->bhkv', k_dec, vc)
        return state, o_inter + o_intra

    s0 = jnp.zeros((bsz, nh, dk, dv), f32)
    _, out = lax.scan(step, s0, (to_chunks(q.astype(f32)), to_chunks(k),
                                 to_chunks(v.astype(f32)), gcum))
    return jnp.moveaxis(out, 0, 1).reshape(bsz, seq, nh, dv)


def hybrid_layer(x, pre_g, post_g, w_in, b_merge,
                 gm_ln_g, gm_ln_b, gm_w_s, gm_b_s,
                 lru_conv_w, lru_conv_b, lru_w_r, lru_b_r, lru_w_i, lru_b_i, lru_lambda,
                 hg_lb, hg_norm_g, w_a_proj, w_b_proj, w_c_proj, w_out):
    bsz, seq, _ = x.shape
    xn = rms_norm(x, pre_g)
    z = jnp.einsum('bsd,dn->bsn', xn, w_in)
    (gm_u, gm_v, gm_gate, lru_x, lru_gate,
     hg_q, hg_f, hg_i, hg_gate, merge) = jnp.split(z, SPLIT_POINTS, axis=-1)

    y_a = chunked_spatial_gating(jax.nn.gelu(gm_u), jax.nn.gelu(gm_v),
                                 gm_ln_g, gm_ln_b, gm_w_s, gm_b_s)
    y_a = jnp.einsum('bsw,wd->bsd', y_a * jax.nn.silu(gm_gate), w_a_proj)

    xb = causal_depthwise_conv(lru_x, lru_conv_w, lru_conv_b)
    y_b = rg_lru(xb, lru_w_r, lru_b_r, lru_w_i, lru_b_i, lru_lambda)
    y_b = jnp.einsum('bsw,wd->bsd', y_b * jax.nn.silu(lru_gate), w_b_proj)

    o = hgrn2_chunked(hg_q.reshape(bsz, seq, HG_HEADS, HG_EXPAND),
                      hg_f.reshape(bsz, seq, HG_HEADS, HG_EXPAND),
                      hg_i.reshape(bsz, seq, HG_HEADS, HG_HEAD_DIM), hg_lb)
    o = o * lax.rsqrt(jnp.mean(o * o, axis=-1, keepdims=True) + EPS)
    o = (o * hg_norm_g.astype(jnp.float32).reshape(HG_HEADS, HG_HEAD_DIM))
    o = o.reshape(bsz, seq, HG_WIDTH).astype(x.dtype)
    y_c = jnp.einsum('bsw,wd->bsd', o * jax.nn.silu(hg_gate), w_c_proj)

    gates = jax.nn.sigmoid(merge.reshape(bsz, seq, N_BRANCH, D_MODEL) + b_merge)
    merged = gates[:, :, 0] * y_a + gates[:, :, 1] * y_b + gates[:, :, 2] * y_c
    y = jnp.einsum('bsd,de->bse', merged, w_out)
    return x + rms_norm(y, post_g)


def setup_inputs(seed: int = 0) -> dict:
    key = jax.random.key(seed)
    ks = jax.random.split(key, 22)
    f32 = jnp.float32
    L, D = DEPTH, D_MODEL
    nrm = lambda k, shape, s: jax.random.normal(k, shape, f32) * s
    dh_lru = LRU_WIDTH // LRU_HEADS
    u = jax.random.uniform(ks[15], (L, LRU_WIDTH), f32, 0.9, 0.999)
    a_base = u ** (1.0 / LRU_C)
    return {
        "x": nrm(ks[0], (BATCH, SEQ, D), 1.0),
        "pre_norm_g": 1.0 + nrm(ks[1], (L, D), 0.02),
        "post_norm_g": 1.0 + nrm(ks[2], (L, D), 0.02),
        "w_in": nrm(ks[3], (L, D, IN_COLS), D ** -0.5),
        "b_merge": nrm(ks[4], (L, N_BRANCH, D), 0.02),
        "gm_ln_g": 1.0 + nrm(ks[5], (L, GM_WIDTH), 0.02),
        "gm_ln_b": nrm(ks[6], (L, GM_WIDTH), 0.02),
        "gm_w_s": nrm(ks[7], (L, GM_GROUPS, GM_CHUNK, GM_CHUNK), GM_CHUNK ** -0.5),
        "gm_b_s": 1.0 + nrm(ks[8], (L, GM_GROUPS, GM_CHUNK), 0.02),
        "lru_conv_w": nrm(ks[9], (L, LRU_CONV, LRU_WIDTH), LRU_CONV ** -0.5),
        "lru_conv_b": nrm(ks[10], (L, LRU_WIDTH), 0.02),
        "lru_w_r": nrm(ks[11], (L, LRU_HEADS, dh_lru, dh_lru), dh_lru ** -0.5),
        "lru_b_r": nrm(ks[12], (L, LRU_WIDTH), 0.02),
        "lru_w_i": nrm(ks[13], (L, LRU_HEADS, dh_lru, dh_lru), dh_lru ** -0.5),
        "lru_b_i": nrm(ks[14], (L, LRU_WIDTH), 0.02),
        "lru_lambda": jnp.log(a_base) - jnp.log1p(-a_base),
        "hg_lower_bounds": nrm(ks[16], (L, HG_KEY_WIDTH), 0.1),
        "hg_norm_g": 1.0 + nrm(ks[17], (L, HG_WIDTH), 0.02),
        "w_a_proj": nrm(ks[18], (L, GM_WIDTH, D), GM_WIDTH ** -0.5),
        "w_b_proj": nrm(ks[19], (L, LRU_WIDTH, D), LRU_WIDTH ** -0.5),
        "w_c_proj": nrm(ks[20], (L, HG_WIDTH, D), HG_WIDTH ** -0.5),
        "w_out": nrm(ks[21], (L, D, D), D ** -0.5),
    }


def reference(x, pre_norm_g, post_norm_g, w_in, b_merge, gm_ln_g, gm_ln_b, gm_w_s, gm_b_s,
              lru_conv_w, lru_conv_b, lru_w_r, lru_b_r, lru_w_i, lru_b_i, lru_lambda,
              hg_lower_bounds, hg_norm_g, w_a_proj, w_b_proj, w_c_proj, w_out):
    p = jax.nn.softmax(hg_lower_bounds.astype(jnp.float32), axis=0)
    lower_bounds = jnp.cumsum(p, axis=0) - p[0]
    h = x
    for l in range(DEPTH):
        h = hybrid_layer(h, pre_norm_g[l], post_norm_g[l], w_in[l], b_merge[l],
                         gm_ln_g[l], gm_ln_b[l], gm_w_s[l], gm_b_s[l],
                         lru_conv_w[l], lru_conv_b[l], lru_w_r[l], lru_b_r[l],
                         lru_w_i[l], lru_b_i[l], lru_lambda[l],
                         lower_bounds[l], hg_norm_g[l],
                         w_a_proj[l], w_b_proj[l], w_c_proj[l], w_out[l])
    return h
```

```python
import functools

import jax
import jax.numpy as jnp
from jax import lax
from jax.experimental import pallas as pl
from jax.experimental.pallas import tpu as pltpu

D_MODEL = 1024
NUM_HEADS = 8
HEAD_DIM = D_MODEL // NUM_HEADS
GM_CHUNK = 128
HG_CHUNK = 128
LRU_CONV = 4
LRU_C = 8.0
EPS = 1e-6
SUBLANES = 8
TOKEN_TILE = 256
VMEM_LIMIT_BYTES = 56 * 1024 * 1024

SEC_GM_U, SEC_GM_V, SEC_GM_GATE = 0, 1, 2
SEC_LRU_X, SEC_LRU_GATE = 3, 4
SEC_HG_Q, SEC_HG_F, SEC_HG_I, SEC_HG_GATE = 5, 6, 7, 8
SEC_MERGE = 9

F32 = jnp.float32
BF16 = jnp.bfloat16


def _dot(a, b):
    return jnp.dot(a, b, preferred_element_type=F32)


def _dot_nt(a, b):
    return lax.dot_general(a, b, (((1,), (1,)), ((), ())), preferred_element_type=F32)


def _dot_tn(a, b):
    return lax.dot_general(a, b, (((0,), (0,)), ((), ())), preferred_element_type=F32)


def _rms_norm(x, g):
    return x * lax.rsqrt(jnp.mean(x * x, axis=-1, keepdims=True) + EPS) * g


def _layer_norm(x, g, b):
    mu = jnp.mean(x, axis=-1, keepdims=True)
    xc = x - mu
    return xc * lax.rsqrt(jnp.mean(xc * xc, axis=-1, keepdims=True) + EPS) * g + b


def _head(a, h):
    return a[:, h * HEAD_DIM:(h + 1) * HEAD_DIM]


def _branch_a_kernel(l_ref, x_ref, pre_g_ref, wu_ref, wv_ref, wg_ref, wm_ref, bm_ref,
                     ln_g_ref, ln_b_ref, ws_ref, bs_ref, wa_ref, out_ref):
    del l_ref
    tile = x_ref.shape[0]
    n_chunks = tile // GM_CHUNK
    xn = _rms_norm(x_ref[...], pre_g_ref[...]).astype(BF16)
    u = jax.nn.gelu(_dot(xn, wu_ref[...]))
    v = jax.nn.gelu(_dot(xn, wv_ref[...]))
    vb = _layer_norm(v, ln_g_ref[...], ln_b_ref[...]).astype(BF16)

    ti = lax.broadcasted_iota(jnp.int32, (GM_CHUNK, GM_CHUNK), 0)
    si = lax.broadcasted_iota(jnp.int32, (GM_CHUNK, GM_CHUNK), 1)
    causal = ti >= si
    per_group = []
    for g in range(NUM_HEADS):
        w = jnp.where(causal, ws_ref[g], 0.0).astype(BF16)
        rhs = jnp.concatenate(
            [_head(vb[c * GM_CHUNK:(c + 1) * GM_CHUNK], g) for c in range(n_chunks)], axis=1)
        per_group.append(_dot(w, rhs))
    mixed = jnp.concatenate(
        [jnp.concatenate([_head(per_group[g], c) for g in range(NUM_HEADS)], axis=1)
         for c in range(n_chunks)], axis=0)
    mixed = mixed + jnp.concatenate([bs_ref[...]] * n_chunks, axis=0)

    gate = _dot(xn, wg_ref[...])
    ya = (u * mixed * jax.nn.silu(gate)).astype(BF16)
    y = _dot(ya, wa_ref[...])
    merge = jax.nn.sigmoid(_dot(xn, wm_ref[...]) + bm_ref[...])
    out_ref[...] = merge * y


def _branch_b_kernel(l_ref, x_ref, acc_ref, pre_g_ref, wx_ref, wg_ref, wm_ref, bm_ref,
                     cw_ref, cb_ref, wri_ref, br_ref, bi_ref, lam_ref, wb_ref, out_ref,
                     conv_buf, h_carry, a_buf, b_buf, h_buf):
    del l_ref
    tile = x_ref.shape[0]

    @pl.when(pl.program_id(1) == 0)
    def _():
        conv_buf[0:SUBLANES, :] = jnp.zeros((SUBLANES, D_MODEL), F32)
        h_carry[...] = jnp.zeros_like(h_carry)

    xn = _rms_norm(x_ref[...], pre_g_ref[...]).astype(BF16)
    lx = _dot(xn, wx_ref[...])

    conv_buf[SUBLANES:SUBLANES + tile, :] = lx
    cw = cw_ref[...]
    xb = lx * cw[LRU_CONV - 1:LRU_CONV] + cb_ref[...]
    for k in range(1, LRU_CONV):
        xb = xb + conv_buf[SUBLANES - k:SUBLANES - k + tile, :] * cw[LRU_CONV - 1 - k:LRU_CONV - k]
    conv_buf[0:SUBLANES, :] = conv_buf[tile:tile + SUBLANES, :]

    xbb = xb.astype(BF16)
    ri = [_dot(_head(xbb, h), wri_ref[h]) for h in range(NUM_HEADS)]
    r_pre = jnp.concatenate([p[:, :HEAD_DIM] for p in ri], axis=1)
    i_pre = jnp.concatenate([p[:, HEAD_DIM:] for p in ri], axis=1)
    r = jax.nn.sigmoid(r_pre + br_ref[...])
    i = jax.nn.sigmoid(i_pre + bi_ref[...])
    log_a = (-LRU_C * jax.nn.softplus(-lam_ref[...])) * r
    a = jnp.exp(log_a)
    a_buf[...] = a
    b_buf[...] = jnp.sqrt(1.0 - a * a) * (i * xb)

    row = lax.broadcasted_iota(jnp.int32, (SUBLANES, D_MODEL), 0)

    def block(n, carry):
        r0 = pl.multiple_of(n * SUBLANES, SUBLANES)
        ab = a_buf[pl.ds(r0, SUBLANES), :]
        bb = b_buf[pl.ds(r0, SUBLANES), :]
        for d in (1, 2, 4):
            keep = row >= d
            bb = bb + ab * jnp.where(keep, pltpu.roll(bb, d, 0), 0.0)
            ab = ab * jnp.where(keep, pltpu.roll(ab, d, 0), 1.0)
        hb = ab * carry + bb
        h_buf[pl.ds(r0, SUBLANES), :] = hb
        return hb[SUBLANES - 1:SUBLANES, :]

    h_carry[0:1, :] = lax.fori_loop(0, tile // SUBLANES, block, h_carry[0:1, :])

    gate = _dot(xn, wg_ref[...])
    yb = (h_buf[...] * jax.nn.silu(gate)).astype(BF16)
    y = _dot(yb, wb_ref[...])
    merge = jax.nn.sigmoid(_dot(xn, wm_ref[...]) + bm_ref[...])
    out_ref[...] = acc_ref[...] + merge * y


def _level_codes():
    ti = lax.broadcasted_iota(jnp.int32, (HG_CHUNK, HG_CHUNK), 0)
    si = lax.broadcasted_iota(jnp.int32, (HG_CHUNK, HG_CHUNK), 1)
    diff = ti ^ si
    code = jnp.zeros((HG_CHUNK, HG_CHUNK), jnp.int32)
    m = 1
    while m < HG_CHUNK:
        code = code + (diff >= m).astype(jnp.int32)
        m *= 2
    return jnp.where(ti >= si, code, -1)


def _level_factors(gc, f):
    n = gc.shape[0]
    row = lax.broadcasted_iota(jnp.int32, gc.shape, 0)
    f_prev = pltpu.roll(f, 1, 0)
    f_next = pltpu.roll(f, n - 1, 0)
    r2 = row & 3
    out = [jnp.where((row & 1) == 1, f, 1.0),
           jnp.where(r2 == 0, f_next, jnp.where(r2 == 1, 1.0, jnp.where(r2 == 2, f, f * f_prev)))]
    m = 4
    while m < n:
        anchor = jnp.concatenate(
            [jnp.broadcast_to(gc[g0 + m - 1:g0 + m, :], (2 * m, gc.shape[1])) for g0 in range(0, n, 2 * m)],
            axis=0)
        out.append(jnp.exp(-jnp.abs(gc - anchor)))
        m *= 2
    return out


def _branch_c_kernel(l_ref, x_ref, acc_ref, pre_g_ref, wq_ref, wf_ref, wi_ref, wg_ref, wm_ref, bm_ref,
                     lb_ref, ng_ref, wc_ref, wo_ref, post_g_ref, out_ref, state):
    tile = x_ref.shape[0]
    n_chunks = tile // HG_CHUNK

    @pl.when(pl.program_id(1) == 0)
    def _():
        state[...] = jnp.zeros_like(state)

    x = x_ref[...]
    xn = _rms_norm(x, pre_g_ref[...]).astype(BF16)
    q = _dot(xn, wq_ref[...])
    fr = _dot(xn, wf_ref[...])
    v = _dot(xn, wi_ref[...])

    raw = lb_ref[...]
    e = jnp.exp(raw - jnp.max(raw, axis=0, keepdims=True))
    p = e / jnp.sum(e, axis=0, keepdims=True)
    layer_row = lax.broadcasted_iota(jnp.int32, raw.shape, 0)
    lb = jnp.sum(jnp.where(layer_row <= l_ref[0], p, 0.0), axis=0, keepdims=True) - p[0:1]

    f = lb + (1.0 - lb) * jax.nn.sigmoid(fr)
    kk = 1.0 - f
    g = jnp.log(f)

    ti = lax.broadcasted_iota(jnp.int32, (HG_CHUNK, HG_CHUNK), 0)
    si = lax.broadcasted_iota(jnp.int32, (HG_CHUNK, HG_CHUNK), 1)
    tri = jnp.where(ti >= si, 1.0, 0.0).astype(BF16)
    g_hi = g.astype(BF16)
    g_r1 = g - g_hi.astype(F32)
    g_mid = g_r1.astype(BF16)
    g_lo = (g_r1 - g_mid.astype(F32)).astype(BF16)

    codes = _level_codes()
    qb = q.astype(BF16)
    kb = kk.astype(BF16)
    vb = v.astype(BF16)

    o_chunks = []
    for c in range(n_chunks):
        rows = slice(c * HG_CHUNK, (c + 1) * HG_CHUNK)
        gc = _dot(tri, g_hi[rows]) + _dot(tri, g_mid[rows]) + _dot(tri, g_lo[rows])
        g_last = gc[HG_CHUNK - 1:HG_CHUNK, :]
        q_inter = (q[rows] * jnp.exp(gc)).astype(BF16)
        k_state = (kk[rows] * jnp.exp(g_last - gc)).astype(BF16)
        state_decay = jnp.exp(g_last)
        factors = _level_factors(gc, f[rows])
        q_lv = [(q[rows] * fac).astype(BF16) for fac in factors]
        k_lv = [(kk[rows] * fac).astype(BF16) for fac in factors]
        o_heads = []
        for h in range(NUM_HEADS):
            scores = jnp.where(codes == 0, _dot_nt(_head(qb[rows], h), _head(kb[rows], h)), 0.0)
            for lv in range(len(factors)):
                scores = jnp.where(codes == lv + 1, _dot_nt(_head(q_lv[lv], h), _head(k_lv[lv], h)), scores)
            v_h = _head(vb[rows], h)
            st = state[h]
            o_h = _dot(scores.astype(BF16), v_h) + _dot_nt(_head(q_inter, h), st.astype(BF16))
            state[h] = st * _head(state_decay, h) + _dot_tn(v_h, _head(k_state, h))
            o_heads.append(o_h * lax.rsqrt(jnp.mean(o_h * o_h, axis=-1, keepdims=True) + EPS))
        o_chunks.append(jnp.concatenate(o_heads, axis=1))
    o = jnp.concatenate(o_chunks, axis=0) * ng_ref[...]

    gate = _dot(xn, wg_ref[...])
    yc = _dot((o * jax.nn.silu(gate)).astype(BF16), wc_ref[...])
    merge = jax.nn.sigmoid(_dot(xn, wm_ref[...]) + bm_ref[...])
    merged = acc_ref[...] + merge * yc
    y = _dot(merged.astype(BF16), wo_ref[...])
    out_ref[...] = x + _rms_norm(y, post_g_ref[...])


def _token_spec(tile):
    return pl.BlockSpec((None, tile, D_MODEL), lambda b, j, l: (b, j, 0))


def _w_in_spec(section):
    return pl.BlockSpec((None, D_MODEL, D_MODEL), lambda b, j, l, s=section: (l[0], 0, s),
                        pipeline_mode=pl.Buffered(1))


def _layer_spec(*shape):
    zeros = (0,) * len(shape)
    return pl.BlockSpec((None,) + shape, lambda b, j, l: (l[0],) + zeros, pipeline_mode=pl.Buffered(1))


def _full_spec(*shape):
    zeros = (0,) * len(shape)
    return pl.BlockSpec(shape, lambda b, j, l: zeros, pipeline_mode=pl.Buffered(1))


def _call(body, in_specs, scratch_shapes, semantics, batch, seq, tile):
    return pl.pallas_call(
        body,
        name=body.__name__.strip("_"),
        grid_spec=pltpu.PrefetchScalarGridSpec(
            num_scalar_prefetch=1,
            grid=(batch, seq // tile),
            in_specs=in_specs,
            out_specs=_token_spec(tile),
            scratch_shapes=scratch_shapes),
        out_shape=jax.ShapeDtypeStruct((batch, seq, D_MODEL), F32),
        compiler_params=pltpu.CompilerParams(
            dimension_semantics=semantics, vmem_limit_bytes=VMEM_LIMIT_BYTES),
    )


def kernel(x, pre_norm_g, post_norm_g, w_in, b_merge, gm_ln_g, gm_ln_b, gm_w_s, gm_b_s, lru_conv_w, lru_conv_b, lru_w_r, lru_b_r, lru_w_i, lru_b_i, lru_lambda, hg_lower_bounds, hg_norm_g, w_a_proj, w_b_proj, w_c_proj, w_out):
    batch, seq, d = x.shape
    depth = w_in.shape[0]
    tile = TOKEN_TILE
    assert d == D_MODEL and seq % tile == 0 and tile % GM_CHUNK == 0 and tile % HG_CHUNK == 0
    assert w_in.shape[2] == (SEC_MERGE + 3) * D_MODEL

    row = lambda a: a.reshape(depth, 1, D_MODEL)
    w_in_b = w_in.astype(BF16)
    wa_b, wb_b, wc_b, wo_b = (w.astype(BF16) for w in (w_a_proj, w_b_proj, w_c_proj, w_out))
    w_ri_b = jnp.concatenate([lru_w_r, lru_w_i], axis=-1).astype(BF16)
    bs_map = jnp.repeat(jnp.swapaxes(gm_b_s, 1, 2), HEAD_DIM, axis=2)
    vec = _layer_spec(1, D_MODEL)
    mat = _layer_spec(D_MODEL, D_MODEL)

    call_a = _call(
        _branch_a_kernel,
        [_token_spec(tile), vec, _w_in_spec(SEC_GM_U), _w_in_spec(SEC_GM_V), _w_in_spec(SEC_GM_GATE),
         _w_in_spec(SEC_MERGE + 0), vec, vec, vec,
         _layer_spec(NUM_HEADS, GM_CHUNK, GM_CHUNK), _layer_spec(GM_CHUNK, D_MODEL), mat],
        [], ("parallel", "parallel"), batch, seq, tile)
    call_b = _call(
        _branch_b_kernel,
        [_token_spec(tile), _token_spec(tile), vec, _w_in_spec(SEC_LRU_X), _w_in_spec(SEC_LRU_GATE),
         _w_in_spec(SEC_MERGE + 1), vec, _layer_spec(LRU_CONV, D_MODEL), vec,
         _layer_spec(NUM_HEADS, HEAD_DIM, 2 * HEAD_DIM), vec, vec, vec, mat],
        [pltpu.VMEM((tile + 2 * SUBLANES, D_MODEL), F32), pltpu.VMEM((SUBLANES, D_MODEL), F32),
         pltpu.VMEM((tile, D_MODEL), F32), pltpu.VMEM((tile, D_MODEL), F32), pltpu.VMEM((tile, D_MODEL), F32)],
        ("parallel", "arbitrary"), batch, seq, tile)
    call_c = _call(
        _branch_c_kernel,
        [_token_spec(tile), _token_spec(tile), vec, _w_in_spec(SEC_HG_Q), _w_in_spec(SEC_HG_F),
         _w_in_spec(SEC_HG_I), _w_in_spec(SEC_HG_GATE), _w_in_spec(SEC_MERGE + 2), vec,
         _full_spec(depth, D_MODEL), vec, mat, mat, vec],
        [pltpu.VMEM((NUM_HEADS, HEAD_DIM, HEAD_DIM), F32)],
        ("parallel", "arbitrary"), batch, seq, tile)

    bm = b_merge.reshape(depth, 3, 1, D_MODEL)

    def layer(l, h):
        li = jnp.reshape(l, (1,)).astype(jnp.int32)
        acc = call_a(li, h, row(pre_norm_g), w_in_b, w_in_b, w_in_b, w_in_b, bm[:, 0],
                     row(gm_ln_g), row(gm_ln_b), gm_w_s, bs_map, wa_b)
        acc = call_b(li, h, acc, row(pre_norm_g), w_in_b, w_in_b, w_in_b, bm[:, 1],
                     lru_conv_w, row(lru_conv_b), w_ri_b, row(lru_b_r), row(lru_b_i), row(lru_lambda), wb_b)
        return call_c(li, h, acc, row(pre_norm_g), w_in_b, w_in_b, w_in_b, w_in_b, w_in_b, bm[:, 2],
                      hg_lower_bounds, row(hg_norm_g), wc_b, wo_b, row(post_norm_g))

    return lax.fori_loop(0, depth, layer, x)
```

```python
import jax
import jax.numpy as jnp
from jax import lax
from jax.experimental import pallas as pl
from jax.experimental.pallas import tpu as pltpu

D_MODEL = 1024
NUM_HEADS = 8
HEAD_DIM = D_MODEL // NUM_HEADS
GM_CHUNK = 128
HG_CHUNK = 128
LRU_CONV = 4
LRU_C = 8.0
EPS = 1e-6
SUBLANES = 8
TOKEN_TILE = 256
VMEM_LIMIT_BYTES = 60 * 1024 * 1024

SEC_GM_U, SEC_GM_V, SEC_GM_GATE = 0, 1, 2
SEC_LRU_X, SEC_LRU_GATE = 3, 4
SEC_HG_Q, SEC_HG_F, SEC_HG_I, SEC_HG_GATE = 5, 6, 7, 8
SEC_MERGE = 9
NUM_SECTIONS = 12

F32 = jnp.float32
BF16 = jnp.bfloat16


def _dot(a, b):
    return jnp.dot(a, b, preferred_element_type=F32)


def _dot_nt(a, b):
    return lax.dot_general(a, b, (((1,), (1,)), ((), ())), preferred_element_type=F32)


def _dot_tn(a, b):
    return lax.dot_general(a, b, (((0,), (0,)), ((), ())), preferred_element_type=F32)


def _rms_norm(x, g):
    return x * lax.rsqrt(jnp.mean(x * x, axis=-1, keepdims=True) + EPS) * g


def _layer_norm(x, g, b):
    mu = jnp.mean(x, axis=-1, keepdims=True)
    xc = x - mu
    return xc * lax.rsqrt(jnp.mean(xc * xc, axis=-1, keepdims=True) + EPS) * g + b


def _head(a, h):
    return a[:, h * HEAD_DIM:(h + 1) * HEAD_DIM]


def _gmlp_branch(xn, proj, ln_g_ref, ln_b_ref, ws_ref, bs_ref, wa_ref):
    tile = xn.shape[0]
    n_chunks = tile // GM_CHUNK
    u = proj(SEC_GM_U)
    yield
    v = proj(SEC_GM_V)
    yield
    u = jax.nn.gelu(u)
    v = jax.nn.gelu(v)
    vb = _layer_norm(v, ln_g_ref[...], ln_b_ref[...]).astype(BF16)
    yield

    ti = lax.broadcasted_iota(jnp.int32, (GM_CHUNK, GM_CHUNK), 0)
    si = lax.broadcasted_iota(jnp.int32, (GM_CHUNK, GM_CHUNK), 1)
    causal = ti >= si
    per_group = []
    for g in range(NUM_HEADS):
        w = jnp.where(causal, ws_ref[g], 0.0).astype(BF16)
        rhs = jnp.concatenate(
            [_head(vb[c * GM_CHUNK:(c + 1) * GM_CHUNK], g) for c in range(n_chunks)], axis=1)
        per_group.append(_dot(w, rhs))
    mixed = jnp.concatenate(
        [jnp.concatenate([_head(per_group[g], c) for g in range(NUM_HEADS)], axis=1)
         for c in range(n_chunks)], axis=0)
    mixed = mixed + jnp.concatenate([bs_ref[...]] * n_chunks, axis=0)
    yield
    gate = proj(SEC_GM_GATE)
    yield
    ya = (u * mixed * jax.nn.silu(gate)).astype(BF16)
    yield
    return _dot(ya, wa_ref[...])


def _lru_branch(xn, proj, cw_ref, cb_ref, wri_ref, br_ref, bi_ref, lam_ref, wb_ref, conv_buf, h_carry):
    tile = xn.shape[0]
    lx = proj(SEC_LRU_X)
    yield

    conv_buf[SUBLANES:SUBLANES + tile, :] = lx
    cw = cw_ref[...]
    xb = lx * cw[LRU_CONV - 1:LRU_CONV] + cb_ref[...]
    for k in range(1, LRU_CONV):
        xb = xb + conv_buf[SUBLANES - k:SUBLANES - k + tile, :] * cw[LRU_CONV - 1 - k:LRU_CONV - k]
    conv_buf[0:SUBLANES, :] = conv_buf[tile:tile + SUBLANES, :]

    xbb = xb.astype(BF16)
    ri = [_dot(_head(xbb, h), wri_ref[h]) for h in range(NUM_HEADS)]
    r_pre = jnp.concatenate([p[:, :HEAD_DIM] for p in ri], axis=1)
    i_pre = jnp.concatenate([p[:, HEAD_DIM:] for p in ri], axis=1)
    yield
    r = jax.nn.sigmoid(r_pre + br_ref[...])
    i = jax.nn.sigmoid(i_pre + bi_ref[...])
    log_a = (-LRU_C * jax.nn.softplus(-lam_ref[...])) * r
    a = jnp.exp(log_a)
    bx = jnp.sqrt(1.0 - a * a) * (i * xb)
    yield
    gate = proj(SEC_LRU_GATE)
    yield

    n_blocks = tile // SUBLANES
    a3 = a.reshape(n_blocks, SUBLANES, D_MODEL)
    b3 = bx.reshape(n_blocks, SUBLANES, D_MODEL)
    row = lax.broadcasted_iota(jnp.int32, a3.shape, 1)
    for d in (1, 2, 4):
        keep = row >= d
        b3 = b3 + a3 * jnp.where(keep, pltpu.roll(b3, d, 1), 0.0)
        a3 = a3 * jnp.where(keep, pltpu.roll(a3, d, 1), 1.0)
    carry = h_carry[0:1, :]
    h_blocks = []
    for n in range(n_blocks):
        hb = a3[n] * carry + b3[n]
        h_blocks.append(hb)
        carry = hb[SUBLANES - 1:SUBLANES, :]
    h_carry[0:1, :] = carry
    h_seq = jnp.concatenate(h_blocks, axis=0)
    yield
    yb = (h_seq * jax.nn.silu(gate)).astype(BF16)
    yield
    return _dot(yb, wb_ref[...])


def _level_codes():
    ti = lax.broadcasted_iota(jnp.int32, (HG_CHUNK, HG_CHUNK), 0)
    si = lax.broadcasted_iota(jnp.int32, (HG_CHUNK, HG_CHUNK), 1)
    diff = ti ^ si
    code = jnp.zeros((HG_CHUNK, HG_CHUNK), jnp.int32)
    m = 1
    while m < HG_CHUNK:
        code = code + (diff >= m).astype(jnp.int32)
        m *= 2
    return jnp.where(ti >= si, code, -1)


def _level_factors(gc, f):
    n = gc.shape[0]
    row = lax.broadcasted_iota(jnp.int32, gc.shape, 0)
    f_prev = pltpu.roll(f, 1, 0)
    f_next = pltpu.roll(f, n - 1, 0)
    r2 = row & 3
    out = [jnp.where((row & 1) == 1, f, 1.0),
           jnp.where(r2 == 0, f_next, jnp.where(r2 == 1, 1.0, jnp.where(r2 == 2, f, f * f_prev)))]
    m = 4
    while m < n:
        anchor = jnp.concatenate(
            [jnp.broadcast_to(gc[g0 + m - 1:g0 + m, :], (2 * m, gc.shape[1])) for g0 in range(0, n, 2 * m)],
            axis=0)
        out.append(jnp.exp(-jnp.abs(gc - anchor)))
        m *= 2
    return out


def _hgrn2_branch(xn, proj, layer, lb_ref, ng_ref, wc_ref, state):
    tile = xn.shape[0]
    n_chunks = tile // HG_CHUNK
    fr = proj(SEC_HG_F)
    yield
    q = proj(SEC_HG_Q)
    yield

    raw = lb_ref[...]
    e = jnp.exp(raw - jnp.max(raw, axis=0, keepdims=True))
    p = e / jnp.sum(e, axis=0, keepdims=True)
    layer_row = lax.broadcasted_iota(jnp.int32, raw.shape, 0)
    lb = jnp.sum(jnp.where(layer_row <= layer, p, 0.0), axis=0, keepdims=True) - p[0:1]

    f = lb + (1.0 - lb) * jax.nn.sigmoid(fr)
    kk = 1.0 - f
    g = jnp.log(f)

    ti = lax.broadcasted_iota(jnp.int32, (HG_CHUNK, HG_CHUNK), 0)
    si = lax.broadcasted_iota(jnp.int32, (HG_CHUNK, HG_CHUNK), 1)
    tri = jnp.where(ti >= si, 1.0, 0.0).astype(BF16)
    g_hi = g.astype(BF16)
    g_r1 = g - g_hi.astype(F32)
    g_mid = g_r1.astype(BF16)
    g_lo = (g_r1 - g_mid.astype(F32)).astype(BF16)
    kb = kk.astype(BF16)
    yield
    v = proj(SEC_HG_I)
    yield
    gcs = [_dot(tri, g_hi[c * HG_CHUNK:(c + 1) * HG_CHUNK]) + _dot(tri, g_mid[c * HG_CHUNK:(c + 1) * HG_CHUNK])
           + _dot(tri, g_lo[c * HG_CHUNK:(c + 1) * HG_CHUNK]) for c in range(n_chunks)]
    yield

    n_blocks = HG_CHUNK // SUBLANES
    codes = _level_codes()
    code_blk = [codes[j * SUBLANES:(j + 1) * SUBLANES] for j in range(n_blocks)]
    n_levels = HG_CHUNK.bit_length() - 1
    level_blocks = []
    for lv in range(n_levels):
        m = 1 << lv
        level_blocks.append([j for j in range(n_blocks) if m < SUBLANES or (j * SUBLANES) & m])
    level_masks = [[code_blk[j] == lv + 1 for j in level_blocks[lv]] for lv in range(n_levels)]
    diag_masks = [code_blk[j] == 0 for j in range(n_blocks)]

    def take_blocks(a, blocks):
        if len(blocks) == n_blocks:
            return a
        return jnp.concatenate([a[j * SUBLANES:(j + 1) * SUBLANES] for j in blocks], axis=0)

    qb = q.astype(BF16)
    vb = v.astype(BF16)

    o_chunks = []
    for c in range(n_chunks):
        rows = slice(c * HG_CHUNK, (c + 1) * HG_CHUNK)
        q_c, k_c = q[rows], kk[rows]
        gc = gcs[c]
        g_last = gc[HG_CHUNK - 1:HG_CHUNK, :]
        q_inter = (q_c * jnp.exp(gc)).astype(BF16)
        k_state = (k_c * jnp.exp(g_last - gc)).astype(BF16)
        state_decay = jnp.exp(g_last)
        factors = _level_factors(gc, f[rows])
        q_lv = [(take_blocks(q_c, level_blocks[lv]) * take_blocks(factors[lv], level_blocks[lv])).astype(BF16)
                for lv in range(n_levels)]
        k_lv = [(k_c * fac).astype(BF16) for fac in factors]
        yield
        o_heads = []
        for h in range(NUM_HEADS):
            p_diag = _dot_nt(_head(qb[rows], h), _head(kb[rows], h))
            blk = [jnp.where(diag_masks[j], p_diag[j * SUBLANES:(j + 1) * SUBLANES], 0.0) for j in range(n_blocks)]
            for lv in range(n_levels):
                p_lv = _dot_nt(_head(q_lv[lv], h), _head(k_lv[lv], h))
                for i, j in enumerate(level_blocks[lv]):
                    blk[j] = jnp.where(level_masks[lv][i], p_lv[i * SUBLANES:(i + 1) * SUBLANES], blk[j])
            scores = jnp.concatenate(blk, axis=0)
            v_h = _head(vb[rows], h)
            st = state[h]
            o_h = _dot(scores.astype(BF16), v_h) + _dot_nt(_head(q_inter, h), st.astype(BF16))
            state[h] = st * _head(state_decay, h) + _dot_tn(v_h, _head(k_state, h))
            o_heads.append(o_h * lax.rsqrt(jnp.mean(o_h * o_h, axis=-1, keepdims=True) + EPS))
        o_chunks.append(jnp.concatenate(o_heads, axis=1))
        yield
    gate = proj(SEC_HG_GATE)
    yield
    yc = (jnp.concatenate(o_chunks, axis=0) * ng_ref[...] * jax.nn.silu(gate)).astype(BF16)
    yield
    return _dot(yc, wc_ref[...])


PHASE_ORDER = "cccccbcacacbcmbbbcamamcabmmamabcm"


def _run_phases(branches, order):
    results = {}
    for k in order:
        try:
            next(branches[k])
        except StopIteration as stop:
            results[k] = stop.value
    assert sorted(results) == sorted(branches), "PHASE_ORDER must run every branch to completion"
    return results


def _layer_kernel(l_ref, x_ref, pre_g_ref, w_in_ref, bm_ref,
                  ln_g_ref, ln_b_ref, ws_ref, bs_ref, wa_ref,
                  cw_ref, cb_ref, wri_ref, br_ref, bi_ref, lam_ref, wb_ref,
                  lb_ref, ng_ref, wc_ref, wo_ref, post_g_ref, out_ref,
                  conv_buf, h_carry, state):
    @pl.when(pl.program_id(1) == 0)
    def _():
        conv_buf[0:SUBLANES, :] = jnp.zeros((SUBLANES, D_MODEL), F32)
        h_carry[...] = jnp.zeros_like(h_carry)
        state[...] = jnp.zeros_like(state)

    x = x_ref[...]
    xn = _rms_norm(x, pre_g_ref[...]).astype(BF16)

    def proj(section):
        return _dot(xn, w_in_ref[:, section * D_MODEL:(section + 1) * D_MODEL])

    def merge_gates():
        gates = []
        for branch in range(3):
            pre = proj(SEC_MERGE + branch)
            yield
            gates.append(jax.nn.sigmoid(pre + bm_ref[branch]))
            yield
        return gates

    out = _run_phases(
        {"a": _gmlp_branch(xn, proj, ln_g_ref, ln_b_ref, ws_ref, bs_ref, wa_ref),
         "b": _lru_branch(xn, proj, cw_ref, cb_ref, wri_ref, br_ref, bi_ref, lam_ref, wb_ref, conv_buf, h_carry),
         "c": _hgrn2_branch(xn, proj, l_ref[0], lb_ref, ng_ref, wc_ref, state),
         "m": merge_gates()},
        PHASE_ORDER)
    gates = out["m"]
    merged = gates[0] * out["a"] + gates[1] * out["b"] + gates[2] * out["c"]
    y = _dot(merged.astype(BF16), wo_ref[...])
    out_ref[...] = x + _rms_norm(y, post_g_ref[...])


def _token_spec(tile):
    return pl.BlockSpec((None, tile, D_MODEL), lambda b, j, l: (b, j, 0))


def _layer_spec(*shape):
    zeros = (0,) * len(shape)
    return pl.BlockSpec((None,) + shape, lambda b, j, l: (l[0],) + zeros, pipeline_mode=pl.Buffered(1))


def _full_spec(*shape):
    zeros = (0,) * len(shape)
    return pl.BlockSpec(shape, lambda b, j, l: zeros, pipeline_mode=pl.Buffered(1))


def kernel(x, pre_norm_g, post_norm_g, w_in, b_merge, gm_ln_g, gm_ln_b, gm_w_s, gm_b_s, lru_conv_w, lru_conv_b, lru_w_r, lru_b_r, lru_w_i, lru_b_i, lru_lambda, hg_lower_bounds, hg_norm_g, w_a_proj, w_b_proj, w_c_proj, w_out):
    batch, seq, d = x.shape
    depth = w_in.shape[0]
    tile = TOKEN_TILE
    assert d == D_MODEL and seq % tile == 0 and tile % GM_CHUNK == 0 and tile % HG_CHUNK == 0
    assert w_in.shape[2] == NUM_SECTIONS * D_MODEL

    row = lambda a: a.reshape(depth, 1, D_MODEL)
    w_in_b = w_in.astype(BF16)
    wa_b, wb_b, wc_b, wo_b = (w.astype(BF16) for w in (w_a_proj, w_b_proj, w_c_proj, w_out))
    w_ri_b = jnp.concatenate([lru_w_r, lru_w_i], axis=-1).astype(BF16)
    bs_map = jnp.repeat(jnp.swapaxes(gm_b_s, 1, 2), HEAD_DIM, axis=2)
    vec = _layer_spec(1, D_MODEL)
    mat = _layer_spec(D_MODEL, D_MODEL)

    layer_call = pl.pallas_call(
        _layer_kernel,
        name="hybrid_layer",
        grid_spec=pltpu.PrefetchScalarGridSpec(
            num_scalar_prefetch=1,
            grid=(batch, seq // tile),
            in_specs=[
                _token_spec(tile), vec, _layer_spec(D_MODEL, NUM_SECTIONS * D_MODEL), _layer_spec(3, 1, D_MODEL),
                vec, vec, _layer_spec(NUM_HEADS, GM_CHUNK, GM_CHUNK), _layer_spec(GM_CHUNK, D_MODEL), mat,
                _layer_spec(LRU_CONV, D_MODEL), vec, _layer_spec(NUM_HEADS, HEAD_DIM, 2 * HEAD_DIM), vec, vec, vec, mat,
                _full_spec(depth, D_MODEL), vec, mat, mat, vec],
            out_specs=_token_spec(tile),
            scratch_shapes=[
                pltpu.VMEM((tile + 2 * SUBLANES, D_MODEL), F32),
                pltpu.VMEM((SUBLANES, D_MODEL), F32),
                pltpu.VMEM((NUM_HEADS, HEAD_DIM, HEAD_DIM), F32)]),
        out_shape=jax.ShapeDtypeStruct((batch, seq, D_MODEL), F32),
        input_output_aliases={1: 0},
        compiler_params=pltpu.CompilerParams(
            dimension_semantics=("parallel", "arbitrary"), vmem_limit_bytes=VMEM_LIMIT_BYTES),
    )

    bm = b_merge.reshape(depth, 3, 1, D_MODEL)

    def layer(l, h):
        li = jnp.reshape(l, (1,)).astype(jnp.int32)
        return layer_call(li, h, row(pre_norm_g), w_in_b, bm,
                          row(gm_ln_g), row(gm_ln_b), gm_w_s, bs_map, wa_b,
                          lru_conv_w, row(lru_conv_b), w_ri_b, row(lru_b_r), row(lru_b_i), row(lru_lambda), wb_b,
                          hg_lower_bounds, row(hg_norm_g), wc_b, wo_b, row(post_norm_g))

    return lax.fori_loop(0, depth, layer, x)
```

```python
import jax
import jax.numpy as jnp
from jax import lax
from jax.experimental import pallas as pl
from jax.experimental.pallas import tpu as pltpu

D_MODEL = 1024
NUM_HEADS = 8
HEAD_DIM = D_MODEL // NUM_HEADS
GM_CHUNK = 128
HG_CHUNK = 128
LRU_CONV = 4
LRU_C = 8.0
EPS = 1e-6
SUBLANES = 8
COL_BLOCK = 2 * HEAD_DIM
N_COL_BLOCKS = D_MODEL // COL_BLOCK
HEADS_PER_BLOCK = COL_BLOCK // HEAD_DIM
TOKEN_TILE = 256
VMEM_LIMIT_BYTES = 60 * 1024 * 1024

SEC_GM_U, SEC_GM_V, SEC_GM_GATE = 0, 1, 2
SEC_LRU_X, SEC_LRU_GATE = 3, 4
SEC_HG_Q, SEC_HG_F, SEC_HG_I, SEC_HG_GATE = 5, 6, 7, 8
SEC_MERGE = 9
NUM_SECTIONS = 12

F32 = jnp.float32
BF16 = jnp.bfloat16

MXU, VPU = "mxu", "vpu"
COST_PROJ = 256


def _dot(a, b):
    return jnp.dot(a, b, preferred_element_type=F32)


def _dot_nt(a, b):
    return lax.dot_general(a, b, (((1,), (1,)), ((), ())), preferred_element_type=F32)


def _dot_tn(a, b):
    return lax.dot_general(a, b, (((0,), (0,)), ((), ())), preferred_element_type=F32)


def _rms_norm(x, g):
    return x * lax.rsqrt(jnp.mean(x * x, axis=-1, keepdims=True) + EPS) * g


def _head(a, h):
    return a[:, h * HEAD_DIM:(h + 1) * HEAD_DIM]


def _cols(cb):
    return slice(cb * COL_BLOCK, (cb + 1) * COL_BLOCK)


def _project(operand_blocks, w_ref):
    operand = jnp.concatenate(operand_blocks, axis=1)
    pieces = []
    for cb in range(N_COL_BLOCKS):
        yield MXU, COST_PROJ
        pieces.append(_dot(operand, w_ref[:, _cols(cb)]))
    return pieces


def _gmlp_branch(proj, tile, ln_g_ref, ln_b_ref, ws_ref, bs_ref, wa_ref):
    n_chunks = tile // GM_CHUNK
    u, v = [], []
    row_sum = row_sq = 0.0
    for cb in range(N_COL_BLOCKS):
        yield MXU, COST_PROJ
        u_pre = proj(SEC_GM_U, cb)
        yield MXU, COST_PROJ
        v_pre = proj(SEC_GM_V, cb)
        yield VPU, 420
        u.append(jax.nn.gelu(u_pre))
        vg = jax.nn.gelu(v_pre)
        v.append(vg)
        row_sum = row_sum + (_head(vg, 0) + _head(vg, 1))
        row_sq = row_sq + (_head(vg, 0) * _head(vg, 0) + _head(vg, 1) * _head(vg, 1))
    yield VPU, 60
    mean = jnp.sum(row_sum, axis=-1, keepdims=True) * (1.0 / D_MODEL)
    var = jnp.sum(row_sq, axis=-1, keepdims=True) * (1.0 / D_MODEL) - mean * mean
    inv = lax.rsqrt(var + EPS)

    ti = lax.broadcasted_iota(jnp.int32, (GM_CHUNK, GM_CHUNK), 0)
    si = lax.broadcasted_iota(jnp.int32, (GM_CHUNK, GM_CHUNK), 1)
    causal = ti >= si
    operands = []
    for cb in range(N_COL_BLOCKS):
        yield VPU, 200
        vb = ((v[cb] - mean) * inv * ln_g_ref[:, _cols(cb)] + ln_b_ref[:, _cols(cb)]).astype(BF16)
        mixed_groups = []
        for hl in range(HEADS_PER_BLOCK):
            w = jnp.where(causal, ws_ref[cb * HEADS_PER_BLOCK + hl], 0.0).astype(BF16)
            rhs = jnp.concatenate(
                [_head(vb[c * GM_CHUNK:(c + 1) * GM_CHUNK], hl) for c in range(n_chunks)], axis=1)
            res = _dot(w, rhs)
            mixed_groups.append(jnp.concatenate([_head(res, c) for c in range(n_chunks)], axis=0))
        mixed = jnp.concatenate(mixed_groups, axis=1) + jnp.concatenate([bs_ref[:, _cols(cb)]] * n_chunks, axis=0)
        yield MXU, COST_PROJ
        gate = proj(SEC_GM_GATE, cb)
        yield VPU, 130
        operands.append((u[cb] * mixed * jax.nn.silu(gate)).astype(BF16))
    return (yield from _project(operands, wa_ref))


def _lru_branch(proj, tile, cw_ref, cb_ref, wri_ref, br_ref, bi_ref, lam_ref, wb_ref, conv_buf, h_carry):
    n_blocks = tile // SUBLANES
    row = lax.broadcasted_iota(jnp.int32, (n_blocks, SUBLANES, COL_BLOCK), 1)
    operands = []
    for cb in range(N_COL_BLOCKS):
        cols = _cols(cb)
        yield MXU, COST_PROJ
        lx = proj(SEC_LRU_X, cb)
        yield VPU, 400
        conv_buf[SUBLANES:SUBLANES + tile, cols] = lx
        xb = lx * cw_ref[LRU_CONV - 1:LRU_CONV, cols] + cb_ref[:, cols]
        for k in range(1, LRU_CONV):
            xb = xb + conv_buf[SUBLANES - k:SUBLANES - k + tile, cols] * cw_ref[LRU_CONV - 1 - k:LRU_CONV - k, cols]
        conv_buf[0:SUBLANES, cols] = conv_buf[tile:tile + SUBLANES, cols]
        xbb = xb.astype(BF16)
        ri = [_dot(_head(xbb, hl), wri_ref[cb * HEADS_PER_BLOCK + hl]) for hl in range(HEADS_PER_BLOCK)]
        r_pre = jnp.concatenate([p[:, :HEAD_DIM] for p in ri], axis=1)
        i_pre = jnp.concatenate([p[:, HEAD_DIM:] for p in ri], axis=1)
        yield VPU, 450
        r = jax.nn.sigmoid(r_pre + br_ref[:, cols])
        i = jax.nn.sigmoid(i_pre + bi_ref[:, cols])
        log_a = (-LRU_C * jax.nn.softplus(-lam_ref[:, cols])) * r
        a = jnp.exp(log_a)
        bx = jnp.sqrt(1.0 - a * a) * (i * xb)
        yield MXU, COST_PROJ
        gate = proj(SEC_LRU_GATE, cb)
        yield VPU, 370
        a3 = a.reshape(n_blocks, SUBLANES, COL_BLOCK)
        b3 = bx.reshape(n_blocks, SUBLANES, COL_BLOCK)
        for d in (1, 2, 4):
            keep = row >= d
            b3 = b3 + a3 * jnp.where(keep, pltpu.roll(b3, d, 1), 0.0)
            a3 = a3 * jnp.where(keep, pltpu.roll(a3, d, 1), 1.0)
        carry = h_carry[0:1, cols]
        h_blocks = []
        for n in range(n_blocks):
            hb = a3[n] * carry + b3[n]
            h_blocks.append(hb)
            carry = hb[SUBLANES - 1:SUBLANES, :]
        h_carry[0:1, cols] = carry
        yield VPU, 130
        operands.append((jnp.concatenate(h_blocks, axis=0) * jax.nn.silu(gate)).astype(BF16))
    return (yield from _project(operands, wb_ref))


def _level_codes():
    ti = lax.broadcasted_iota(jnp.int32, (HG_CHUNK, HG_CHUNK), 0)
    si = lax.broadcasted_iota(jnp.int32, (HG_CHUNK, HG_CHUNK), 1)
    diff = ti ^ si
    code = jnp.zeros((HG_CHUNK, HG_CHUNK), jnp.int32)
    m = 1
    while m < HG_CHUNK:
        code = code + (diff >= m).astype(jnp.int32)
        m *= 2
    return jnp.where(ti >= si, code, -1)


def _level_factors(gc, f):
    n = gc.shape[0]
    row = lax.broadcasted_iota(jnp.int32, gc.shape, 0)
    f_prev = pltpu.roll(f, 1, 0)
    f_next = pltpu.roll(f, n - 1, 0)
    r2 = row & 3
    out = [jnp.where((row & 1) == 1, f, 1.0),
           jnp.where(r2 == 0, f_next, jnp.where(r2 == 1, 1.0, jnp.where(r2 == 2, f, f * f_prev)))]
    m = 4
    while m < n:
        anchor = jnp.concatenate(
            [jnp.broadcast_to(gc[g0 + m - 1:g0 + m, :], (2 * m, gc.shape[1])) for g0 in range(0, n, 2 * m)],
            axis=0)
        out.append(jnp.exp2(-jnp.abs(gc - anchor)))
        m *= 2
    return out


def _hgrn2_branch(proj, tile, layer, lb_ref, ng_ref, wc_ref, state):
    n_chunks = tile // HG_CHUNK
    yield VPU, 60
    raw = lb_ref[...]
    e = jnp.exp(raw - jnp.max(raw, axis=0, keepdims=True))
    p = e / jnp.sum(e, axis=0, keepdims=True)
    layer_row = lax.broadcasted_iota(jnp.int32, raw.shape, 0)
    lb_all = jnp.sum(jnp.where(layer_row <= layer, p, 0.0), axis=0, keepdims=True) - p[0:1]

    ti = lax.broadcasted_iota(jnp.int32, (HG_CHUNK, HG_CHUNK), 0)
    si = lax.broadcasted_iota(jnp.int32, (HG_CHUNK, HG_CHUNK), 1)
    tri = jnp.where(ti >= si, 1.0, 0.0).astype(BF16)

    n_blocks = HG_CHUNK // SUBLANES
    codes = _level_codes()
    code_blk = [codes[j * SUBLANES:(j + 1) * SUBLANES] for j in range(n_blocks)]
    n_levels = HG_CHUNK.bit_length() - 1
    level_blocks = []
    for lv in range(n_levels):
        m = 1 << lv
        level_blocks.append([j for j in range(n_blocks) if m < SUBLANES or (j * SUBLANES) & m])
    level_masks = [[code_blk[j] == lv + 1 for j in level_blocks[lv]] for lv in range(n_levels)]
    diag_masks = [code_blk[j] == 0 for j in range(n_blocks)]

    def take_blocks(a, blocks):
        if len(blocks) == n_blocks:
            return a
        return jnp.concatenate([a[j * SUBLANES:(j + 1) * SUBLANES] for j in blocks], axis=0)

    operands = []
    for cb in range(N_COL_BLOCKS):
        cols = _cols(cb)
        yield MXU, COST_PROJ
        fr = proj(SEC_HG_F, cb)
        yield MXU, COST_PROJ
        q = proj(SEC_HG_Q, cb)
        yield VPU, 480
        lb = lb_all[:, cols]
        f = lb + (1.0 - lb) * jax.nn.sigmoid(fr)
        kk = 1.0 - f
        g = jnp.log2(f)
        g_hi = g.astype(BF16)
        g_r1 = g - g_hi.astype(F32)
        g_mid = g_r1.astype(BF16)
        g_lo = (g_r1 - g_mid.astype(F32)).astype(BF16)
        qb = q.astype(BF16)
        yield MXU, COST_PROJ
        v = proj(SEC_HG_I, cb)
        vb = v.astype(BF16)
        yield MXU, 200
        gcs = []
        for c in range(n_chunks):
            rows = slice(c * HG_CHUNK, (c + 1) * HG_CHUNK)
            gcs.append(_dot(tri, g_hi[rows]) + _dot(tri, g_mid[rows]) + _dot(tri, g_lo[rows]))

        o_chunks = []
        for c in range(n_chunks):
            rows = slice(c * HG_CHUNK, (c + 1) * HG_CHUNK)
            yield VPU, 300
            q_c, k_c = q[rows], kk[rows]
            gc = gcs[c]
            g_last = gc[HG_CHUNK - 1:HG_CHUNK, :]
            q_inter = (q_c * jnp.exp2(gc)).astype(BF16)
            k_state = (k_c * jnp.exp2(g_last - gc)).astype(BF16)
            state_decay = jnp.exp2(g_last)
            factors = _level_factors(gc, f[rows])
            q_lv = [(take_blocks(q_c, level_blocks[lv]) * take_blocks(factors[lv], level_blocks[lv])).astype(BF16)
                    for lv in range(n_levels)]
            k_lv_t = [(k_c * fac).T.astype(BF16) for fac in factors]
            k_t = k_c.T.astype(BF16)
            yield MXU, 300
            heads = range(HEADS_PER_BLOCK)
            head_rows = [slice(hl * HEAD_DIM, (hl + 1) * HEAD_DIM) for hl in heads]
            st_idx = [cb * HEADS_PER_BLOCK + hl for hl in heads]
            p_diag = [_dot(_head(qb[rows], hl), k_t[head_rows[hl]]) for hl in heads]
            p_lv = [[_dot(_head(q_lv[lv], hl), k_lv_t[lv][head_rows[hl]]) for lv in range(n_levels)] for hl in heads]
            o_inter = [_dot_nt(_head(q_inter, hl), state[st_idx[hl]].astype(BF16)) for hl in heads]
            scores = []
            for hl in heads:
                blk = [jnp.where(diag_masks[j], p_diag[hl][j * SUBLANES:(j + 1) * SUBLANES], 0.0)
                       for j in range(n_blocks)]
                for lv in range(n_levels):
                    for i, j in enumerate(level_blocks[lv]):
                        blk[j] = jnp.where(level_masks[lv][i], p_lv[hl][lv][i * SUBLANES:(i + 1) * SUBLANES], blk[j])
                scores.append(jnp.concatenate(blk, axis=0).astype(BF16))
            o_h = [_dot(scores[hl], _head(vb[rows], hl)) + o_inter[hl] for hl in heads]
            for hl in heads:
                state[st_idx[hl]] = (state[st_idx[hl]] * _head(state_decay, hl)
                                     + _dot_tn(_head(vb[rows], hl), _head(k_state, hl)))
            o_chunks.append(jnp.concatenate(
                [o_h[hl] * lax.rsqrt(jnp.mean(o_h[hl] * o_h[hl], axis=-1, keepdims=True) + EPS) for hl in heads],
                axis=1))
        yield MXU, COST_PROJ
        gate = proj(SEC_HG_GATE, cb)
        yield VPU, 130
        operands.append((jnp.concatenate(o_chunks, axis=0) * ng_ref[:, cols] * jax.nn.silu(gate)).astype(BF16))
    return (yield from _project(operands, wc_ref))


def _merge_gates(proj, bm_ref):
    gates = [[], [], []]
    for cb in range(N_COL_BLOCKS):
        for branch in range(3):
            yield MXU, COST_PROJ
            pre = proj(SEC_MERGE + branch, cb)
            yield VPU, 110
            gates[branch].append(jax.nn.sigmoid(pre + bm_ref[branch][:, _cols(cb)]))
    return gates


def _run_phases(branches):
    pending = {k: next(g) for k, g in branches.items()}
    spent = {MXU: 0, VPU: 0}
    last_traced = {k: -1 for k in branches}
    results = {}
    step = 0
    while pending:
        want = MXU if spent[MXU] <= spent[VPU] else VPU
        candidates = [k for k in pending if pending[k][0] == want] or list(pending)
        k = min(candidates, key=lambda name: last_traced[name])
        last_traced[k] = step
        step += 1
        kind, cost = pending[k]
        spent[kind] += cost
        try:
            pending[k] = next(branches[k])
        except StopIteration as stop:
            results[k] = stop.value
            del pending[k]
    return results


def _layer_kernel(l_ref, x_ref, pre_g_ref, w_in_ref, bm_ref,
                  ln_g_ref, ln_b_ref, ws_ref, bs_ref, wa_ref,
                  cw_ref, cb_ref, wri_ref, br_ref, bi_ref, lam_ref, wb_ref,
                  lb_ref, ng_ref, wc_ref, wo_ref, post_g_ref, out_ref,
                  conv_buf, h_carry, state):
    @pl.when(pl.program_id(1) == 0)
    def _():
        conv_buf[0:SUBLANES, :] = jnp.zeros((SUBLANES, D_MODEL), F32)
        h_carry[...] = jnp.zeros_like(h_carry)
        state[...] = jnp.zeros_like(state)

    tile = x_ref.shape[0]
    x = x_ref[...]
    xn = _rms_norm(x, pre_g_ref[...]).astype(BF16)

    def proj(section, cb):
        start = section * D_MODEL + cb * COL_BLOCK
        return _dot(xn, w_in_ref[:, start:start + COL_BLOCK])

    out = _run_phases({
        "hgrn2": _hgrn2_branch(proj, tile, l_ref[0], lb_ref, ng_ref, wc_ref, state),
        "lru": _lru_branch(proj, tile, cw_ref, cb_ref, wri_ref, br_ref, bi_ref, lam_ref, wb_ref, conv_buf, h_carry),
        "gmlp": _gmlp_branch(proj, tile, ln_g_ref, ln_b_ref, ws_ref, bs_ref, wa_ref),
        "gates": _merge_gates(proj, bm_ref),
    })
    gates = out["gates"]
    merged = jnp.concatenate(
        [(gates[0][cb] * out["gmlp"][cb] + gates[1][cb] * out["lru"][cb] + gates[2][cb] * out["hgrn2"][cb]).astype(BF16)
         for cb in range(N_COL_BLOCKS)], axis=1)
    y = _dot(merged, wo_ref[...])
    out_ref[...] = x + _rms_norm(y, post_g_ref[...])


def _token_spec(tile):
    return pl.BlockSpec((None, tile, D_MODEL), lambda b, j, l: (b, j, 0))


def _layer_spec(*shape):
    zeros = (0,) * len(shape)
    return pl.BlockSpec((None,) + shape, lambda b, j, l: (l[0],) + zeros, pipeline_mode=pl.Buffered(1))


def _full_spec(*shape):
    zeros = (0,) * len(shape)
    return pl.BlockSpec(shape, lambda b, j, l: zeros, pipeline_mode=pl.Buffered(1))


def kernel(x, pre_norm_g, post_norm_g, w_in, b_merge, gm_ln_g, gm_ln_b, gm_w_s, gm_b_s, lru_conv_w, lru_conv_b, lru_w_r, lru_b_r, lru_w_i, lru_b_i, lru_lambda, hg_lower_bounds, hg_norm_g, w_a_proj, w_b_proj, w_c_proj, w_out):
    batch, seq, d = x.shape
    depth = w_in.shape[0]
    tile = TOKEN_TILE
    assert d == D_MODEL and seq % tile == 0 and tile % GM_CHUNK == 0 and tile % HG_CHUNK == 0
    assert w_in.shape[2] == NUM_SECTIONS * D_MODEL

    row = lambda a: a.reshape(depth, 1, D_MODEL)
    w_in_b = w_in.astype(BF16)
    wa_b, wb_b, wc_b, wo_b = (w.astype(BF16) for w in (w_a_proj, w_b_proj, w_c_proj, w_out))
    w_ri_b = jnp.concatenate([lru_w_r, lru_w_i], axis=-1).astype(BF16)
    bs_map = jnp.repeat(jnp.swapaxes(gm_b_s, 1, 2), HEAD_DIM, axis=2)
    vec = _layer_spec(1, D_MODEL)
    mat = _layer_spec(D_MODEL, D_MODEL)

    layer_call = pl.pallas_call(
        _layer_kernel,
        name="hybrid_layer",
        grid_spec=pltpu.PrefetchScalarGridSpec(
            num_scalar_prefetch=1,
            grid=(batch, seq // tile),
            in_specs=[
                _token_spec(tile), vec, _layer_spec(D_MODEL, NUM_SECTIONS * D_MODEL), _layer_spec(3, 1, D_MODEL),
                vec, vec, _layer_spec(NUM_HEADS, GM_CHUNK, GM_CHUNK), _layer_spec(GM_CHUNK, D_MODEL), mat,
                _layer_spec(LRU_CONV, D_MODEL), vec, _layer_spec(NUM_HEADS, HEAD_DIM, 2 * HEAD_DIM), vec, vec, vec, mat,
                _full_spec(depth, D_MODEL), vec, mat, mat, vec],
            out_specs=_token_spec(tile),
            scratch_shapes=[
                pltpu.VMEM((tile + 2 * SUBLANES, D_MODEL), F32),
                pltpu.VMEM((SUBLANES, D_MODEL), F32),
                pltpu.VMEM((NUM_HEADS, HEAD_DIM, HEAD_DIM), F32)]),
        out_shape=jax.ShapeDtypeStruct((batch, seq, D_MODEL), F32),
        input_output_aliases={1: 0},
        compiler_params=pltpu.CompilerParams(
            dimension_semantics=("parallel", "arbitrary"), vmem_limit_bytes=VMEM_LIMIT_BYTES),
    )

    bm = b_merge.reshape(depth, 3, 1, D_MODEL)

    def layer(l, h):
        li = jnp.reshape(l, (1,)).astype(jnp.int32)
        return layer_call(li, h, row(pre_norm_g), w_in_b, bm,
                          row(gm_ln_g), row(gm_ln_b), gm_w_s, bs_map, wa_b,
                          lru_conv_w, row(lru_conv_b), w_ri_b, row(lru_b_r), row(lru_b_i), row(lru_lambda), wb_b,
                          hg_lower_bounds, row(hg_norm_g), wc_b, wo_b, row(post_norm_g))

    return lax.fori_loop(0, depth, layer, x)
```

```python
import jax
import jax.numpy as jnp
from jax import lax
from jax.experimental import pallas as pl
from jax.experimental.pallas import tpu as pltpu

D_MODEL = 1024
NUM_HEADS = 8
HEAD_DIM = D_MODEL // NUM_HEADS
GM_CHUNK = 128
HG_CHUNK = 128
LRU_CONV = 4
LRU_C = 8.0
EPS = 1e-6
SUBLANES = 8
COL_BLOCK = 2 * HEAD_DIM
N_COL_BLOCKS = D_MODEL // COL_BLOCK
HEADS_PER_BLOCK = COL_BLOCK // HEAD_DIM
TOKEN_TILE = 256
VMEM_LIMIT_BYTES = 60 * 1024 * 1024

SEC_GM_U, SEC_GM_V, SEC_GM_GATE = 0, 1, 2
SEC_LRU_X, SEC_LRU_GATE = 3, 4
SEC_HG_Q, SEC_HG_F, SEC_HG_I, SEC_HG_GATE = 5, 6, 7, 8
SEC_MERGE = 9
NUM_SECTIONS = 12

F32 = jnp.float32
BF16 = jnp.bfloat16

MXU, VPU = "mxu", "vpu"
COST_PROJ = 256
MXU_PHASE_VPU_SHARE = 0.2
MXU_VPU_BALANCE = 1.25


def _dot(a, b):
    return jnp.dot(a, b, preferred_element_type=F32)


def _dot_nt(a, b):
    return lax.dot_general(a, b, (((1,), (1,)), ((), ())), preferred_element_type=F32)


def _dot_tn(a, b):
    return lax.dot_general(a, b, (((0,), (0,)), ((), ())), preferred_element_type=F32)


def _rms_norm(x, g):
    return x * lax.rsqrt(jnp.mean(x * x, axis=-1, keepdims=True) + EPS) * g


def _head(a, h):
    return a[:, h * HEAD_DIM:(h + 1) * HEAD_DIM]


def _cols(cb):
    return slice(cb * COL_BLOCK, (cb + 1) * COL_BLOCK)


def _project(operand_blocks, w_ref):
    operand = jnp.concatenate(operand_blocks, axis=1)
    pieces = []
    for cb in range(N_COL_BLOCKS):
        yield MXU, COST_PROJ
        pieces.append(_dot(operand, w_ref[:, _cols(cb)]))
    return pieces


def _gmlp_branch(proj, tile, ln_g_ref, ln_b_ref, ws_ref, bs_ref, wa_ref):
    n_chunks = tile // GM_CHUNK
    u, v = [], []
    row_sum = row_sq = 0.0
    for cb in range(N_COL_BLOCKS):
        yield MXU, COST_PROJ
        u_pre = proj(SEC_GM_U, cb)
        yield MXU, COST_PROJ
        v_pre = proj(SEC_GM_V, cb)
        yield VPU, 370
        u.append(jax.nn.gelu(u_pre))
        vg = jax.nn.gelu(v_pre)
        v.append(vg)
        row_sum = row_sum + (_head(vg, 0) + _head(vg, 1))
        row_sq = row_sq + (_head(vg, 0) * _head(vg, 0) + _head(vg, 1) * _head(vg, 1))
    yield VPU, 60
    mean = jnp.sum(row_sum, axis=-1, keepdims=True) * (1.0 / D_MODEL)
    var = jnp.sum(row_sq, axis=-1, keepdims=True) * (1.0 / D_MODEL) - mean * mean
    inv = lax.rsqrt(var + EPS)

    ti = lax.broadcasted_iota(jnp.int32, (GM_CHUNK, GM_CHUNK), 0)
    si = lax.broadcasted_iota(jnp.int32, (GM_CHUNK, GM_CHUNK), 1)
    causal = ti >= si
    operands = []
    for cb in range(N_COL_BLOCKS):
        yield VPU, 110
        vb = ((v[cb] - mean) * inv * ln_g_ref[:, _cols(cb)] + ln_b_ref[:, _cols(cb)]).astype(BF16)
        mixed_groups = []
        for hl in range(HEADS_PER_BLOCK):
            w = jnp.where(causal, ws_ref[cb * HEADS_PER_BLOCK + hl], 0.0).astype(BF16)
            rhs = jnp.concatenate(
                [_head(vb[c * GM_CHUNK:(c + 1) * GM_CHUNK], hl) for c in range(n_chunks)], axis=1)
            res = _dot(w, rhs)
            mixed_groups.append(jnp.concatenate([_head(res, c) for c in range(n_chunks)], axis=0))
        mixed = jnp.concatenate(mixed_groups, axis=1) + jnp.concatenate([bs_ref[:, _cols(cb)]] * n_chunks, axis=0)
        yield MXU, COST_PROJ
        gate = proj(SEC_GM_GATE, cb)
        yield VPU, 130
        operands.append((u[cb] * mixed * jax.nn.silu(gate)).astype(BF16))
    return (yield from _project(operands, wa_ref))


def _segment_permutation(tile):
    seg_len = tile // SUBLANES
    r = lax.broadcasted_iota(jnp.int32, (tile, tile), 0)
    c = lax.broadcasted_iota(jnp.int32, (tile, tile), 1)
    shift = SUBLANES.bit_length() - 1
    perm = jnp.where((r & (SUBLANES - 1)) * seg_len + (r >> shift) == c, 1.0, 0.0).astype(BF16)
    perm_t = jnp.where((c & (SUBLANES - 1)) * seg_len + (c >> shift) == r, 1.0, 0.0).astype(BF16)
    return perm, perm_t


def _lru_branch(proj, unpermute, tile, cw_ref, cb_ref, wri_ref, br_ref, bi_ref, lam_ref, wb_ref, conv_buf, h_carry):
    n_blocks = tile // SUBLANES
    sublane = lax.broadcasted_iota(jnp.int32, (SUBLANES, COL_BLOCK), 0)
    tail_blocks = LRU_CONV - 1
    block = lambda a, i: a[i * SUBLANES:(i + 1) * SUBLANES]
    operands = []
    for cb in range(N_COL_BLOCKS):
        cols = _cols(cb)
        yield MXU, COST_PROJ
        lx = proj(SEC_LRU_X, cb)
        yield VPU, 130
        tails = []
        for n in range(tail_blocks):
            cur = block(lx, n_blocks - tail_blocks + n)
            prev = conv_buf[n * SUBLANES:(n + 1) * SUBLANES, cols]
            tails.append(jnp.where(sublane == 0, pltpu.roll(prev, 1, 0), pltpu.roll(cur, 1, 0)))
        conv_buf[:, cols] = lx[tile - tail_blocks * SUBLANES:]
        xb = lx * cw_ref[LRU_CONV - 1:LRU_CONV, cols] + cb_ref[:, cols]
        for k in range(1, LRU_CONV):
            shifted = jnp.concatenate(tails[tail_blocks - k:] + [lx[:tile - k * SUBLANES]], axis=0)
            xb = xb + shifted * cw_ref[LRU_CONV - 1 - k:LRU_CONV - k, cols]
        xbb = xb.astype(BF16)
        ri = [_dot(_head(xbb, hl), wri_ref[cb * HEADS_PER_BLOCK + hl]) for hl in range(HEADS_PER_BLOCK)]
        r_pre = jnp.concatenate([p[:, :HEAD_DIM] for p in ri], axis=1)
        i_pre = jnp.concatenate([p[:, HEAD_DIM:] for p in ri], axis=1)
        yield VPU, 270
        r = jax.nn.sigmoid(r_pre + br_ref[:, cols])
        i = jax.nn.sigmoid(i_pre + bi_ref[:, cols])
        log_a = (-LRU_C * jax.nn.softplus(-lam_ref[:, cols])) * r
        a = jnp.exp(log_a)
        bx = jnp.sqrt(1.0 - a * a) * (i * xb)
        yield MXU, COST_PROJ
        gate = proj(SEC_LRU_GATE, cb)
        yield VPU, 200
        h_loc, a_run = [block(bx, 0)], [block(a, 0)]
        for n in range(1, n_blocks):
            h_loc.append(block(a, n) * h_loc[-1] + block(bx, n))
            a_run.append(block(a, n) * a_run[-1])
        entering = [h_carry[0:1, cols]]
        for s in range(SUBLANES - 1):
            entering.append(h_loc[-1][s:s + 1] + a_run[-1][s:s + 1] * entering[-1])
        entering = jnp.concatenate(entering, axis=0)
        h_blocks = [h_loc[n] + a_run[n] * entering for n in range(n_blocks)]
        h_carry[0:1, cols] = h_blocks[-1][SUBLANES - 1:SUBLANES]
        yield VPU, 130
        yb = (jnp.concatenate(h_blocks, axis=0) * jax.nn.silu(gate)).astype(BF16)
        yield MXU, 64
        operands.append(unpermute(yb))
    return (yield from _project(operands, wb_ref))


def _level_codes():
    ti = lax.broadcasted_iota(jnp.int32, (HG_CHUNK, HG_CHUNK), 0)
    si = lax.broadcasted_iota(jnp.int32, (HG_CHUNK, HG_CHUNK), 1)
    diff = ti ^ si
    code = jnp.zeros((HG_CHUNK, HG_CHUNK), jnp.int32)
    m = 1
    while m < HG_CHUNK:
        code = code + (diff >= m).astype(jnp.int32)
        m *= 2
    return jnp.where(ti >= si, code, -1)


def _level_factors(gc, f):
    n = gc.shape[0]
    row = lax.broadcasted_iota(jnp.int32, gc.shape, 0)
    f_prev = pltpu.roll(f, 1, 0)
    f_next = pltpu.roll(f, n - 1, 0)
    r2 = row & 3
    out = [jnp.where((row & 1) == 1, f, 1.0),
           jnp.where(r2 == 0, f_next, jnp.where(r2 == 1, 1.0, jnp.where(r2 == 2, f, f * f_prev)))]
    m = 4
    while m < n:
        anchor = jnp.concatenate(
            [jnp.broadcast_to(gc[g0 + m - 1:g0 + m, :], (2 * m, gc.shape[1])) for g0 in range(0, n, 2 * m)],
            axis=0)
        out.append(jnp.exp2(-jnp.abs(gc - anchor)))
        m *= 2
    return out


def _hgrn2_branch(proj, tile, layer, lb_ref, ng_ref, wc_ref, state):
    n_chunks = tile // HG_CHUNK
    yield VPU, 60
    raw = lb_ref[...]
    e = jnp.exp(raw - jnp.max(raw, axis=0, keepdims=True))
    p = e / jnp.sum(e, axis=0, keepdims=True)
    layer_row = lax.broadcasted_iota(jnp.int32, raw.shape, 0)
    lb_all = jnp.sum(jnp.where(layer_row <= layer, p, 0.0), axis=0, keepdims=True) - p[0:1]

    ti = lax.broadcasted_iota(jnp.int32, (HG_CHUNK, HG_CHUNK), 0)
    si = lax.broadcasted_iota(jnp.int32, (HG_CHUNK, HG_CHUNK), 1)
    tri = jnp.where(ti >= si, 1.0, 0.0).astype(BF16)

    n_blocks = HG_CHUNK // SUBLANES
    codes = _level_codes()
    code_blk = [codes[j * SUBLANES:(j + 1) * SUBLANES] for j in range(n_blocks)]
    n_levels = HG_CHUNK.bit_length() - 1
    level_blocks = []
    for lv in range(n_levels):
        m = 1 << lv
        level_blocks.append([j for j in range(n_blocks) if m < SUBLANES or (j * SUBLANES) & m])
    level_masks = [[code_blk[j] == lv + 1 for j in level_blocks[lv]] for lv in range(n_levels)]
    diag_masks = [code_blk[j] == 0 for j in range(n_blocks)]

    def take_blocks(a, blocks):
        if len(blocks) == n_blocks:
            return a
        return jnp.concatenate([a[j * SUBLANES:(j + 1) * SUBLANES] for j in blocks], axis=0)

    operands = []
    for cb in range(N_COL_BLOCKS):
        cols = _cols(cb)
        yield MXU, COST_PROJ
        fr = proj(SEC_HG_F, cb)
        yield MXU, COST_PROJ
        q = proj(SEC_HG_Q, cb)
        yield VPU, 270
        lb = lb_all[:, cols]
        f = lb + (1.0 - lb) * jax.nn.sigmoid(fr)
        kk = 1.0 - f
        g = jnp.log2(f)
        g_hi = g.astype(BF16)
        g_r1 = g - g_hi.astype(F32)
        g_mid = g_r1.astype(BF16)
        g_lo = (g_r1 - g_mid.astype(F32)).astype(BF16)
        qb = q.astype(BF16)
        yield MXU, COST_PROJ
        v = proj(SEC_HG_I, cb)
        vb = v.astype(BF16)
        yield MXU, 200
        gcs = []
        for c in range(n_chunks):
            rows = slice(c * HG_CHUNK, (c + 1) * HG_CHUNK)
            gcs.append(_dot(tri, g_hi[rows]) + _dot(tri, g_mid[rows]) + _dot(tri, g_lo[rows]))

        o_chunks = []
        for c in range(n_chunks):
            rows = slice(c * HG_CHUNK, (c + 1) * HG_CHUNK)
            yield VPU, 300
            q_c, k_c = q[rows], kk[rows]
            gc = gcs[c]
            g_last = gc[HG_CHUNK - 1:HG_CHUNK, :]
            q_inter = (q_c * jnp.exp2(gc)).astype(BF16)
            k_state = (k_c * jnp.exp2(g_last - gc)).astype(BF16)
            state_decay = jnp.exp2(g_last)
            factors = _level_factors(gc, f[rows])
            q_lv = [(take_blocks(q_c, level_blocks[lv]) * take_blocks(factors[lv], level_blocks[lv])).astype(BF16)
                    for lv in range(n_levels)]
            k_lv_t = [(k_c * fac).T.astype(BF16) for fac in factors]
            k_t = k_c.T.astype(BF16)
            yield MXU, 300
            heads = range(HEADS_PER_BLOCK)
            head_rows = [slice(hl * HEAD_DIM, (hl + 1) * HEAD_DIM) for hl in heads]
            st_idx = [cb * HEADS_PER_BLOCK + hl for hl in heads]
            p_diag = [_dot(_head(qb[rows], hl), k_t[head_rows[hl]]) for hl in heads]
            p_lv = [[_dot(_head(q_lv[lv], hl), k_lv_t[lv][head_rows[hl]]) for lv in range(n_levels)] for hl in heads]
            o_inter = [_dot_nt(_head(q_inter, hl), state[st_idx[hl]].astype(BF16)) for hl in heads]
            scores = []
            for hl in heads:
                blk = [jnp.where(diag_masks[j], p_diag[hl][j * SUBLANES:(j + 1) * SUBLANES], 0.0)
                       for j in range(n_blocks)]
                for lv in range(n_levels):
                    for i, j in enumerate(level_blocks[lv]):
                        blk[j] = jnp.where(level_masks[lv][i], p_lv[hl][lv][i * SUBLANES:(i + 1) * SUBLANES], blk[j])
                scores.append(jnp.concatenate(blk, axis=0).astype(BF16))
            o_h = [_dot(scores[hl], _head(vb[rows], hl)) + o_inter[hl] for hl in heads]
            for hl in heads:
                state[st_idx[hl]] = (state[st_idx[hl]] * _head(state_decay, hl)
                                     + _dot_tn(_head(vb[rows], hl), _head(k_state, hl)))
            o_chunks.append(jnp.concatenate(
                [o_h[hl] * lax.rsqrt(jnp.mean(o_h[hl] * o_h[hl], axis=-1, keepdims=True) + EPS) for hl in heads],
                axis=1))
        yield MXU, COST_PROJ
        gate = proj(SEC_HG_GATE, cb)
        yield VPU, 130
        operands.append((jnp.concatenate(o_chunks, axis=0) * ng_ref[:, cols] * jax.nn.silu(gate)).astype(BF16))
    return (yield from _project(operands, wc_ref))


def _merge_gates(proj, bm_ref):
    gates = [[], [], []]
    for cb in range(N_COL_BLOCKS):
        for branch in range(3):
            yield MXU, COST_PROJ
            pre = proj(SEC_MERGE + branch, cb)
            yield VPU, 110
            gates[branch].append(jax.nn.sigmoid(pre + bm_ref[branch][:, _cols(cb)]))
    return gates


def _run_phases(branches):
    pending = {k: next(g) for k, g in branches.items()}
    spent = {MXU: 0, VPU: 0}
    last_traced = {k: -1 for k in branches}
    results = {}
    step = 0
    while pending:
        want = MXU if spent[MXU] <= MXU_VPU_BALANCE * spent[VPU] else VPU
        candidates = [k for k in pending if pending[k][0] == want] or list(pending)
        k = min(candidates, key=lambda name: last_traced[name])
        last_traced[k] = step
        step += 1
        kind, cost = pending[k]
        spent[kind] += cost
        if kind == MXU:
            spent[VPU] += cost * MXU_PHASE_VPU_SHARE
        try:
            pending[k] = next(branches[k])
        except StopIteration as stop:
            results[k] = stop.value
            del pending[k]
    return results


def _layer_kernel(l_ref, x_ref, pre_g_ref, w_in_ref, bm_ref,
                  ln_g_ref, ln_b_ref, ws_ref, bs_ref, wa_ref,
                  cw_ref, cb_ref, wri_ref, br_ref, bi_ref, lam_ref, wb_ref,
                  lb_ref, ng_ref, wc_ref, wo_ref, post_g_ref, out_ref,
                  conv_buf, h_carry, state):
    @pl.when(pl.program_id(1) == 0)
    def _():
        conv_buf[...] = jnp.zeros_like(conv_buf)
        h_carry[...] = jnp.zeros_like(h_carry)
        state[...] = jnp.zeros_like(state)

    tile = x_ref.shape[0]
    x = x_ref[...]
    xn = _rms_norm(x, pre_g_ref[...]).astype(BF16)

    perm, perm_t = _segment_permutation(tile)
    xn_perm = _dot(perm, xn).astype(BF16)

    def proj(section, cb, lhs=xn):
        start = section * D_MODEL + cb * COL_BLOCK
        return _dot(lhs, w_in_ref[:, start:start + COL_BLOCK])

    def proj_perm(section, cb):
        return proj(section, cb, xn_perm)

    def unpermute(block_bf16):
        return _dot(perm_t, block_bf16).astype(BF16)

    out = _run_phases({
        "gmlp": _gmlp_branch(proj, tile, ln_g_ref, ln_b_ref, ws_ref, bs_ref, wa_ref),
        "lru": _lru_branch(proj_perm, unpermute, tile, cw_ref, cb_ref, wri_ref, br_ref, bi_ref, lam_ref, wb_ref,
                           conv_buf, h_carry),
        "hgrn2": _hgrn2_branch(proj, tile, l_ref[0], lb_ref, ng_ref, wc_ref, state),
        "gates": _merge_gates(proj, bm_ref),
    })
    gates = out["gates"]
    merged = jnp.concatenate(
        [(gates[0][cb] * out["gmlp"][cb] + gates[1][cb] * out["lru"][cb] + gates[2][cb] * out["hgrn2"][cb]).astype(BF16)
         for cb in range(N_COL_BLOCKS)], axis=1)
    y = _dot(merged, wo_ref[...])
    out_ref[...] = x + _rms_norm(y, post_g_ref[...])


def _token_spec(tile):
    return pl.BlockSpec((None, tile, D_MODEL), lambda b, j, l: (b, j, 0))


def _layer_spec(*shape):
    zeros = (0,) * len(shape)
    return pl.BlockSpec((None,) + shape, lambda b, j, l: (l[0],) + zeros, pipeline_mode=pl.Buffered(1))


def _full_spec(*shape):
    zeros = (0,) * len(shape)
    return pl.BlockSpec(shape, lambda b, j, l: zeros, pipeline_mode=pl.Buffered(1))


def kernel(x, pre_norm_g, post_norm_g, w_in, b_merge, gm_ln_g, gm_ln_b, gm_w_s, gm_b_s, lru_conv_w, lru_conv_b, lru_w_r, lru_b_r, lru_w_i, lru_b_i, lru_lambda, hg_lower_bounds, hg_norm_g, w_a_proj, w_b_proj, w_c_proj, w_out):
    batch, seq, d = x.shape
    depth = w_in.shape[0]
    tile = TOKEN_TILE
    assert d == D_MODEL and seq % tile == 0 and tile % GM_CHUNK == 0 and tile % HG_CHUNK == 0
    assert w_in.shape[2] == NUM_SECTIONS * D_MODEL

    row = lambda a: a.reshape(depth, 1, D_MODEL)
    w_in_b = w_in.astype(BF16)
    wa_b, wb_b, wc_b, wo_b = (w.astype(BF16) for w in (w_a_proj, w_b_proj, w_c_proj, w_out))
    w_ri_b = jnp.concatenate([lru_w_r, lru_w_i], axis=-1).astype(BF16)
    bs_map = jnp.repeat(jnp.swapaxes(gm_b_s, 1, 2), HEAD_DIM, axis=2)
    vec = _layer_spec(1, D_MODEL)
    mat = _layer_spec(D_MODEL, D_MODEL)

    layer_call = pl.pallas_call(
        _layer_kernel,
        name="hybrid_layer",
        grid_spec=pltpu.PrefetchScalarGridSpec(
            num_scalar_prefetch=1,
            grid=(batch, seq // tile),
            in_specs=[
                _token_spec(tile), vec, _layer_spec(D_MODEL, NUM_SECTIONS * D_MODEL), _layer_spec(3, 1, D_MODEL),
                vec, vec, _layer_spec(NUM_HEADS, GM_CHUNK, GM_CHUNK), _layer_spec(GM_CHUNK, D_MODEL), mat,
                _layer_spec(LRU_CONV, D_MODEL), vec, _layer_spec(NUM_HEADS, HEAD_DIM, 2 * HEAD_DIM), vec, vec, vec, mat,
                _full_spec(depth, D_MODEL), vec, mat, mat, vec],
            out_specs=_token_spec(tile),
            scratch_shapes=[
                pltpu.VMEM(((LRU_CONV - 1) * SUBLANES, D_MODEL), F32),
                pltpu.VMEM((SUBLANES, D_MODEL), F32),
                pltpu.VMEM((NUM_HEADS, HEAD_DIM, HEAD_DIM), F32)]),
        out_shape=jax.ShapeDtypeStruct((batch, seq, D_MODEL), F32),
        input_output_aliases={1: 0},
        compiler_params=pltpu.CompilerParams(
            dimension_semantics=("parallel", "arbitrary"), vmem_limit_bytes=VMEM_LIMIT_BYTES),
    )

    bm = b_merge.reshape(depth, 3, 1, D_MODEL)

    def layer(l, h):
        li = jnp.reshape(l, (1,)).astype(jnp.int32)
        return layer_call(li, h, row(pre_norm_g), w_in_b, bm,
                          row(gm_ln_g), row(gm_ln_b), gm_w_s, bs_map, wa_b,
                          lru_conv_w, row(lru_conv_b), w_ri_b, row(lru_b_r), row(lru_b_i), row(lru_lambda), wb_b,
                          hg_lower_bounds, row(hg_norm_g), wc_b, wo_b, row(post_norm_g))

    return lax.fori_loop(0, depth, layer, x)
```
